```python
import math, functools
import jax, jax.numpy as jnp
from jax import lax
import numpy as np

D_MODEL = 1024
BATCH = 4
SEQ = 4096
DEPTH = 2
DEC_BATCH = 128
DEC_SEQ = 1
PAST_LEN = 2048
PAGE_SIZE = 128

HEAD_DIM = 64
MIX_WIDTH = D_MODEL
GROUP_WIDTH = MIX_WIDTH // 2
N_HEADS = GROUP_WIDTH // HEAD_DIM
D_FF = 2816
ROPE_THETA = 10000.0
NORM_EPS = 1e-6
ATTN_SCALE = HEAD_DIM ** -0.5
NEG = -1e30
Q_BLOCK = 128

RWKV_LORA_W = 64
RWKV_LORA_A = 64
RWKV_LORA_G = 128
RWKV_GN_EPS = 64e-5
RWKV_SIZES = (GROUP_WIDTH, RWKV_LORA_W, GROUP_WIDTH, GROUP_WIDTH, RWKV_LORA_A, RWKV_LORA_G)
RWKV_COLS = sum(RWKV_SIZES)
MOBA_BLOCK = 256
MOBA_TOPK = 3
EVEN_COLS = RWKV_COLS + 3 * GROUP_WIDTH
DSA_TOPK = 256
IDX_HEADS = 4
IDX_DIM = 64
IDX_W_SCALE = (IDX_HEADS * IDX_DIM) ** -0.5
DSA_SIZES = (GROUP_WIDTH, GROUP_WIDTH, GROUP_WIDTH, IDX_HEADS * IDX_DIM, IDX_DIM, IDX_HEADS)
DSA_COLS = sum(DSA_SIZES)
CONV_W = 4
GDN_CHUNK = 64
GDN_CONV_CH = 3 * GROUP_WIDTH
GDN_SIZES = (GDN_CONV_CH, GROUP_WIDTH, N_HEADS, N_HEADS)
ODD_COLS = DSA_COLS + sum(GDN_SIZES)

N_EVEN = (DEPTH + 1) // 2
N_ODD = DEPTH // 2

kernel_name = 'hybrid_rwkv7_moba_dsa_gdn_macaron_step'


def _split(z, sizes):
    out, o = [], 0
    for s in sizes:
        out.append(z[..., o:o + s])
        o += s
    return out


def _rms_norm(x, g):
    xf = x.astype(jnp.float32)
    y = xf * lax.rsqrt(jnp.mean(xf * xf, axis=-1, keepdims=True) + NORM_EPS)
    return (y * g.astype(jnp.float32)).astype(x.dtype)


def _swiglu(x, w_gate, w_up, w_down):
    return (jax.nn.silu(x @ w_gate) * (x @ w_up)) @ w_down


def _rope(x, pos):
    d = x.shape[-1]
    inv = ROPE_THETA ** (-jnp.arange(0, d, 2, dtype=jnp.float32) / d)
    ang = pos.astype(jnp.float32)[:, None] * inv[None, :]
    cos, sin = jnp.cos(ang)[:, None, :], jnp.sin(ang)[:, None, :]
    xf = x.astype(jnp.float32)
    x1, x2 = xf[..., : d // 2], xf[..., d // 2:]
    return jnp.concatenate([x1 * cos - x2 * sin, x2 * cos + x1 * sin], axis=-1).astype(x.dtype)


def _l2n(x):
    return x * lax.rsqrt(jnp.sum(x * x, axis=-1, keepdims=True) + 1e-6)


def _rwkv7_mix(z, z_prev, s0, i, P):
    f32 = jnp.float32
    N, T, _ = z.shape
    zs = jnp.concatenate([z_prev[:, None, :].astype(z.dtype), z[:, :-1]], axis=1)
    zm = z + (zs - z) * P['rwkv_mu'][i]
    r, wl, k, v, al, gl = _split(zm, RWKV_SIZES)
    w = -jax.nn.softplus(-(P['rwkv_w0'][i] + jnp.tanh(wl) @ P['rwkv_w_up'][i])) - 0.5
    decay = jnp.exp(-jnp.exp(w.astype(f32)))
    a = jax.nn.sigmoid(P['rwkv_a0'][i] + al @ P['rwkv_a_up'][i])
    g = jax.nn.sigmoid(gl) @ P['rwkv_g_up'][i]
    hd = lambda t: t.astype(f32).reshape(N, T, N_HEADS, HEAD_DIM)
    r, k, v, a, decay = hd(r), hd(k), hd(v), hd(a), hd(decay)
    kk = k * P['rwkv_k_k'][i].astype(f32).reshape(N_HEADS, HEAD_DIM)
    kk = kk / jnp.maximum(jnp.sqrt(jnp.sum(kk * kk, axis=-1, keepdims=True)), 1e-12)
    k = k * (1.0 + (a - 1.0) * P['rwkv_k_a'][i].astype(f32).reshape(N_HEADS, HEAD_DIM))

    def step(S, inp):
        r_t, w_t, k_t, v_t, kk_t, a_t = inp
        s_kk = jnp.einsum('nhij,nhj->nhi', S, -kk_t)
        S = (S * w_t[:, :, None, :] + s_kk[..., None] * (kk_t * a_t)[:, :, None, :]
             + v_t[..., None] * k_t[:, :, None, :])
        return S, jnp.einsum('nhij,nhj->nhi', S, r_t)

    tm = lambda t: jnp.swapaxes(t, 0, 1)
    S, y = lax.scan(step, s0.astype(f32), (tm(r), tm(decay), tm(k), tm(v), tm(kk), tm(a)))
    y = tm(y)
    mu = jnp.mean(y, axis=-1, keepdims=True)
    var = jnp.mean(jnp.square(y - mu), axis=-1, keepdims=True)
    y = ((y - mu) * lax.rsqrt(var + RWKV_GN_EPS)).reshape(N, T, GROUP_WIDTH)
    y = y * P['rwkv_gn_w'][i].astype(f32) + P['rwkv_gn_b'][i].astype(f32)
    bonus = jnp.sum(r * k * P['rwkv_r_k'][i].astype(f32), axis=-1, keepdims=True) * v
    y = (y + bonus.reshape(N, T, GROUP_WIDTH)) * g.astype(f32)
    return y.astype(z.dtype), S.astype(s0.dtype), z[:, -1]


def _moba_blocks(k, v):
    N, L, H, D = k.shape
    nb = -(-L // MOBA_BLOCK)
    pad = nb * MOBA_BLOCK - L

    def blocks(t):
        t = jnp.pad(t, ((0, 0), (0, pad), (0, 0), (0, 0)))
        return t.reshape(N, nb, MOBA_BLOCK, H, D).transpose(0, 3, 1, 2, 4)

    kbh, vbh = blocks(k), blocks(v)
    return kbh, vbh, jnp.mean(kbh.astype(jnp.float32), axis=3)


def _moba_attend(q, kbh, vbh, kmean, q_pos):
    f32 = jnp.float32
    N, Q, H, D = q.shape
    nb = kbh.shape[2]
    qf = q.astype(f32)
    own = q_pos // MOBA_BLOCK
    gate = jnp.einsum('nqhd,nhbd->nhqb', qf, kmean)
    gate = jnp.where(jnp.arange(nb)[None, :] < own[:, None], gate, NEG)
    kt = min(MOBA_TOPK, nb)
    _, top = lax.top_k(gate, kt)
    sel_ok = jnp.arange(kt)[None, :] < jnp.minimum(own, kt)[:, None]
    blk = jnp.concatenate([top, jnp.broadcast_to(own[None, None, :, None], (N, H, Q, 1))], axis=-1)
    n_i = jnp.arange(N)[:, None, None, None]
    h_i = jnp.arange(H)[None, :, None, None]
    kg = kbh[n_i, h_i, blk].astype(f32)
    vg = vbh[n_i, h_i, blk].astype(f32)
    own_ok = (own[:, None] * MOBA_BLOCK + jnp.arange(MOBA_BLOCK)[None, :]) <= q_pos[:, None]
    ok = jnp.concatenate([jnp.broadcast_to(sel_ok[:, :, None], (Q, kt, MOBA_BLOCK)), own_ok[:, None, :]], axis=1)
    lg = jnp.einsum('nqhd,nhqsld->nhqsl', qf, kg) * ATTN_SCALE
    lg = jnp.where(ok[None, None], lg, -jnp.inf)
    p = jax.nn.softmax(lg.reshape(N, H, Q, -1), axis=-1).reshape(lg.shape)
    return jnp.einsum('nhqsl,nhqsld->nqhd', p, vg).astype(q.dtype)


def _moba_prompt(q, k, v):
    B, T = q.shape[:2]
    kbh, vbh, kmean = _moba_blocks(k, v)
    nq = T // Q_BLOCK
    qc = jnp.swapaxes(q.reshape((B, nq, Q_BLOCK) + q.shape[2:]), 0, 1)
    pos = jnp.arange(T, dtype=jnp.int32).reshape(nq, Q_BLOCK)
    out = lax.map(lambda a: _moba_attend(a[0], kbh, vbh, kmean, a[1]), (qc, pos))
    return jnp.swapaxes(out, 0, 1).reshape(q.shape)


def _moba_sample(q, k, v, ck, cv, page_table):
    N, Q = q.shape[:2]
    past = lambda c: c[page_table].reshape((N, PAST_LEN) + c.shape[2:])
    k_all = jnp.concatenate([past(ck).astype(k.dtype), k], axis=1)
    v_all = jnp.concatenate([past(cv).astype(v.dtype), v], axis=1)
    kbh, vbh, kmean = _moba_blocks(k_all, v_all)
    return _moba_attend(q, kbh, vbh, kmean, PAST_LEN + jnp.arange(Q, dtype=jnp.int32))


def _index_scores(qi, wi, ki):
    s = jax.nn.relu(jnp.einsum('nqhd,nld->nqhl', qi.astype(jnp.float32), ki.astype(jnp.float32)))
    return jnp.einsum('nqh,nqhl->nql', wi.astype(jnp.float32), s)


def _dsa_select(scores, q_pos, ksel):
    L = scores.shape[-1]
    adm = jnp.arange(L)[None, :] <= q_pos[:, None]
    _, idx = lax.top_k(jnp.where(adm[None], scores, NEG), ksel)
    return idx, idx <= q_pos[None, :, None]


def _sparse_attend(q, kg, vg, ok):
    f32 = jnp.float32
    lg = jnp.einsum('nqhd,nqkhd->nhqk', q.astype(f32), kg.astype(f32)) * ATTN_SCALE
    p = jax.nn.softmax(jnp.where(ok[:, None], lg, -jnp.inf), axis=-1)
    return jnp.einsum('nhqk,nqkhd->nqhd', p, vg.astype(f32)).astype(q.dtype)


def _dsa_prompt(q, k, v, qi, ki, wi):
    B, T = q.shape[:2]
    ksel = min(DSA_TOPK, T // 4)
    nq = T // Q_BLOCK
    blk = lambda t: jnp.swapaxes(t.reshape((B, nq, Q_BLOCK) + t.shape[2:]), 0, 1)
    pos = jnp.arange(T, dtype=jnp.int32).reshape(nq, Q_BLOCK)
    b_i = jnp.arange(B)[:, None, None]

    def one(args):
        qc, qic, wic, pc = args
        idx, ok = _dsa_select(_index_scores(qic, wic, ki), pc, ksel)
        return _sparse_attend(qc, k[b_i, idx], v[b_i, idx], ok)

    out = lax.map(one, (blk(q), blk(qi), blk(wi), pos))
    return jnp.swapaxes(out, 0, 1).reshape(q.shape)


def _dsa_sample(q, k, v, qi, ki, wi, ck, cv, cki, page_table):
    N, Q = q.shape[:2]
    ki_all = jnp.concatenate([cki[page_table].reshape(N, PAST_LEN, IDX_DIM).astype(ki.dtype), ki], axis=1)
    ksel = min(DSA_TOPK, (PAST_LEN + Q) // 4)
    pos = PAST_LEN + jnp.arange(Q, dtype=jnp.int32)
    idx, ok = _dsa_select(_index_scores(qi, wi, ki_all), pos, ksel)
    n_i = jnp.arange(N)[:, None, None]
    in_past = (idx < PAST_LEN)[..., None, None]
    pidx = jnp.minimum(idx, PAST_LEN - 1)
    phys = page_table[n_i, pidx // PAGE_SIZE]
    off = pidx % PAGE_SIZE
    nidx = jnp.clip(idx - PAST_LEN, 0, Q - 1)
    kg = jnp.where(in_past, ck[phys, off].astype(k.dtype), k[n_i, nidx])
    vg = jnp.where(in_past, cv[phys, off].astype(v.dtype), v[n_i, nidx])
    return _sparse_attend(q, kg, vg, ok)


def _chunk_gated_delta(q, k, v, g, beta, s0):
    N, T, H, D = q.shape
    C = min(GDN_CHUNK, T)
    nc = -(-T // C)
    pad = nc * C - T

    def prep(t):
        t = jnp.pad(t, ((0, 0), (0, pad)) + ((0, 0),) * (t.ndim - 2))
        return jnp.moveaxis(t.reshape((N, nc, C) + t.shape[2:]), 3, 1)

    q, k, v, g, beta = prep(q), prep(k), prep(v), prep(g), prep(beta)
    G = jnp.cumsum(g, axis=-1)
    tri = jnp.tril(jnp.ones((C, C), bool))
    strict = jnp.tril(jnp.ones((C, C), bool), -1)
    decay = jnp.exp(jnp.where(tri, G[..., :, None] - G[..., None, :], -jnp.inf))
    kb = k * beta[..., None]
    A = jnp.where(strict, jnp.einsum('nhcid,nhcjd->nhcij', kb, k) * decay, 0.0)
    eye = jnp.broadcast_to(jnp.eye(C, dtype=A.dtype), A.shape)
    Tm = lax.linalg.triangular_solve(eye + A, eye, left_side=True, lower=True, unit_diagonal=True)
    eG = jnp.exp(G)[..., None]
    U = Tm @ (v * beta[..., None])
    W = Tm @ (kb * eG)
    attn = jnp.einsum('nhcid,nhcjd->nhcij', q, k) * decay
    qg = q * eG
    kd = k * jnp.exp(G[..., -1:] - G)[..., None]
    gl = jnp.exp(G[..., -1])

    def step(S, xs):
        U_c, W_c, a_c, qg_c, kd_c, gl_c = xs
        vn = U_c - W_c @ S
        o = qg_c @ S + a_c @ vn
        S = S * gl_c[..., None, None] + jnp.swapaxes(kd_c, -1, -2) @ vn
        return S, o

    cm = lambda t: jnp.moveaxis(t, 2, 0)
    S, o = lax.scan(step, s0, (cm(U), cm(W), cm(attn), cm(qg), cm(kd), cm(gl)))
    o = jnp.moveaxis(jnp.moveaxis(o, 0, 2), 1, 3).reshape(N, nc * C, H, D)[:, :T]
    return o, S


def _gdn_mix(z, buf, s0, i, P):
    f32 = jnp.float32
    N, T, _ = z.shape
    qkv, zg, b, a = _split(z, GDN_SIZES)
    ext = jnp.concatenate([buf.astype(qkv.dtype), qkv], axis=1)
    conv = lax.conv_general_dilated(ext, P['gdn_conv_w'][i][:, None, :].astype(ext.dtype), (1,), 'VALID',
                                    dimension_numbers=('NWC', 'WIO', 'NWC'), feature_group_count=GDN_CONV_CH)
    conv = jax.nn.silu(conv.astype(f32))
    hd = lambda t: t.reshape(N, T, N_HEADS, HEAD_DIM)
    q, k, v = (hd(t) for t in _split(conv, (GROUP_WIDTH,) * 3))
    q = _l2n(q) * HEAD_DIM ** -0.5
    k = _l2n(k)
    beta = jax.nn.sigmoid(b.astype(f32))
    gdec = -jnp.exp(P['gdn_a_log'][i].astype(f32)) * jax.nn.softplus(a.astype(f32) + P['gdn_dt_bias'][i].astype(f32))
    o, S = _chunk_gated_delta(q, k, v, gdec, beta, s0.astype(f32))
    o = o * lax.rsqrt(jnp.mean(o * o, axis=-1, keepdims=True) + NORM_EPS) * P['gdn_norm_w'][i].astype(f32)
    o = o * jax.nn.silu(hd(zg).astype(f32))
    return o.reshape(N, T, GROUP_WIDTH).astype(z.dtype), S.astype(s0.dtype), ext[:, -(CONV_W - 1):]


def _even_mixer(h, pos, i, P, shift0, s0, moba_fn):
    N, T, _ = h.shape
    z = h @ P['ev_w_in'][i]
    y_a, s_new, shift_new = _rwkv7_mix(z[..., :RWKV_COLS], shift0, s0, i, P)
    q, k, v = (t.reshape(N, T, N_HEADS, HEAD_DIM) for t in _split(z[..., RWKV_COLS:], (GROUP_WIDTH,) * 3))
    q, k = _rope(q, pos), _rope(k, pos)
    y_b = moba_fn(q, k, v).reshape(N, T, GROUP_WIDTH)
    y = jnp.concatenate([y_a, y_b], axis=-1) @ P['ev_w_out'][i]
    return y, (s_new, shift_new, k, v)


def _odd_mixer(h, pos, i, P, conv0, s0, dsa_fn):
    N, T, _ = h.shape
    z = h @ P['od_w_in'][i]
    q, k, v, qi, ki, wi = _split(z[..., :DSA_COLS], DSA_SIZES)
    hd = lambda t: t.reshape(N, T, N_HEADS, HEAD_DIM)
    q, k, v = _rope(hd(q), pos), _rope(hd(k), pos), hd(v)
    qi = _rope(qi.reshape(N, T, IDX_HEADS, IDX_DIM), pos)
    ki = _rope(ki[:, :, None, :], pos)[:, :, 0]
    y_c = dsa_fn(q, k, v, qi, ki, wi * IDX_W_SCALE).reshape(N, T, GROUP_WIDTH)
    y_d, s_new, conv_new = _gdn_mix(z[..., DSA_COLS:], conv0, s0, i, P)
    y = jnp.concatenate([y_c, y_d], axis=-1) @ P['od_w_out'][i]
    return y, (k, v, ki, s_new, conv_new)


def _trunk(x, pos, P, C, page_table):
    N = x.shape[0]
    ev_states, od_states = [], []
    for l in range(DEPTH):
        i = l // 2
        x = x + 0.5 * _swiglu(_rms_norm(x, P['norm_w'][l, 0]), P['ffn_gate'][l, 0], P['ffn_up'][l, 0], P['ffn_down'][l, 0])
        h = _rms_norm(x, P['norm_w'][l, 1])
        if l % 2 == 0:
            if C is None:
                y, st = _even_mixer(h, pos, i, P, jnp.zeros((N, RWKV_COLS), x.dtype),
                                    jnp.zeros((N, N_HEADS, HEAD_DIM, HEAD_DIM), x.dtype), _moba_prompt)
            else:
                fn = functools.partial(_moba_sample, ck=C['cache_moba_k'][i], cv=C['cache_moba_v'][i], page_table=page_table)
                y, st = _even_mixer(h, pos, i, P, C['state_rwkv_shift'][i], C['state_rwkv'][i], fn)
            ev_states.append(st)
        else:
            if C is None:
                y, st = _odd_mixer(h, pos, i, P, jnp.zeros((N, CONV_W - 1, GDN_CONV_CH), x.dtype),
                                   jnp.zeros((N, N_HEADS, HEAD_DIM, HEAD_DIM), x.dtype), _dsa_prompt)
            else:
                fn = functools.partial(_dsa_sample, ck=C['cache_dsa_k'][i], cv=C['cache_dsa_v'][i],
                                       cki=C['cache_dsa_kidx'][i], page_table=page_table)
                y, st = _odd_mixer(h, pos, i, P, C['state_gdn_conv'][i], C['state_gdn'][i], fn)
            od_states.append(st)
        x = x + y
        x = x + 0.5 * _swiglu(_rms_norm(x, P['norm_w'][l, 2]), P['ffn_gate'][l, 1], P['ffn_up'][l, 1], P['ffn_down'][l, 1])
    ev = [jnp.stack(c) for c in zip(*ev_states)]
    od = [jnp.stack(c) for c in zip(*od_states)]
    return _rms_norm(x, P['final_norm_w']), ev, od


def setup_inputs(seed: int = 0) -> dict:
    key = jax.random.key(seed)
    ks = iter(jax.random.split(key, 64))
    f32 = jnp.float32

    def nrm(shape, scale=1.0):
        return jax.random.normal(next(ks), shape, f32) * scale

    def unif(shape, lo, hi):
        return jax.random.uniform(next(ks), shape, f32, lo, hi)

    n_pages = PAST_LEN // PAGE_SIZE
    n_phys = (DEC_BATCH * n_pages * 5 + 3) // 4
    page_table = jax.random.permutation(next(ks), n_phys)[: DEC_BATCH * n_pages].reshape(DEC_BATCH, n_pages).astype(jnp.int32)
    dt = jnp.exp(unif((N_ODD, N_HEADS), math.log(1e-3), math.log(1e-1)))
    gdn_dt_bias = dt + jnp.log(-jnp.expm1(-dt))
    return {
        'x_prompt': nrm((BATCH, SEQ, D_MODEL)),
        'x_sample': nrm((DEC_BATCH, DEC_SEQ, D_MODEL)),
        'state_rwkv': nrm((N_EVEN, DEC_BATCH, N_HEADS, HEAD_DIM, HEAD_DIM), 0.3),
        'state_rwkv_shift': nrm((N_EVEN, DEC_BATCH, RWKV_COLS)),
        'cache_moba_k': nrm((N_EVEN, n_phys, PAGE_SIZE, N_HEADS, HEAD_DIM)),
        'cache_moba_v': nrm((N_EVEN, n_phys, PAGE_SIZE, N_HEADS, HEAD_DIM)),
        'cache_dsa_k': nrm((N_ODD, n_phys, PAGE_SIZE, N_HEADS, HEAD_DIM)),
        'cache_dsa_v': nrm((N_ODD, n_phys, PAGE_SIZE, N_HEADS, HEAD_DIM)),
        'cache_dsa_kidx': nrm((N_ODD, n_phys, PAGE_SIZE, IDX_DIM)),
        'state_gdn': nrm((N_ODD, DEC_BATCH, N_HEADS, HEAD_DIM, HEAD_DIM), 0.3),
        'state_gdn_conv': nrm((N_ODD, DEC_BATCH, CONV_W - 1, GDN_CONV_CH)),
        'page_table': page_table,
        'norm_w': 1.0 + nrm((DEPTH, 3, D_MODEL), 0.05),
        'final_norm_w': 1.0 + nrm((D_MODEL,), 0.05),
        'ffn_gate': nrm((DEPTH, 2, D_MODEL, D_FF), D_MODEL ** -0.5),
        'ffn_up': nrm((DEPTH, 2, D_MODEL, D_FF), D_MODEL ** -0.5),
        'ffn_down': nrm((DEPTH, 2, D_FF, D_MODEL), D_FF ** -0.5),
        'ev_w_in': nrm((N_EVEN, D_MODEL, EVEN_COLS), D_MODEL ** -0.5),
        'ev_w_out': nrm((N_EVEN, MIX_WIDTH, D_MODEL), MIX_WIDTH ** -0.5),
        'rwkv_mu': unif((N_EVEN, RWKV_COLS), 0.0, 1.0),
        'rwkv_w0': -1.0 + nrm((N_EVEN, GROUP_WIDTH), 0.5),
        'rwkv_w_up': nrm((N_EVEN, RWKV_LORA_W, GROUP_WIDTH), 0.1),
        'rwkv_a0': nrm((N_EVEN, GROUP_WIDTH), 0.5),
        'rwkv_a_up': nrm((N_EVEN, RWKV_LORA_A, GROUP_WIDTH), 0.1),
        'rwkv_g_up': nrm((N_EVEN, RWKV_LORA_G, GROUP_WIDTH), RWKV_LORA_G ** -0.5),
        'rwkv_k_k': 0.85 + nrm((N_EVEN, GROUP_WIDTH), 0.05),
        'rwkv_k_a': 1.0 + nrm((N_EVEN, GROUP_WIDTH), 0.05),
        'rwkv_r_k': nrm((N_EVEN, N_HEADS, HEAD_DIM), 0.1),
        'rwkv_gn_w': 1.0 + nrm((N_EVEN, GROUP_WIDTH), 0.05),
        'rwkv_gn_b': nrm((N_EVEN, GROUP_WIDTH), 0.02),
        'od_w_in': nrm((N_ODD, D_MODEL, ODD_COLS), D_MODEL ** -0.5),
        'od_w_out': nrm((N_ODD, MIX_WIDTH, D_MODEL), MIX_WIDTH ** -0.5),
        'gdn_conv_w': nrm((N_ODD, CONV_W, GDN_CONV_CH), CONV_W ** -0.5),
        'gdn_a_log': jnp.log(unif((N_ODD, N_HEADS), 1.0, 16.0)),
        'gdn_dt_bias': gdn_dt_bias,
        'gdn_norm_w': 1.0 + nrm((N_ODD, HEAD_DIM), 0.05),
    }


def reference(x_prompt, x_sample, state_rwkv, state_rwkv_shift, cache_moba_k, cache_moba_v,
              cache_dsa_k, cache_dsa_v, cache_dsa_kidx, state_gdn, state_gdn_conv, page_table,
              norm_w, final_norm_w, ffn_gate, ffn_up, ffn_down,
              ev_w_in, ev_w_out, rwkv_mu, rwkv_w0, rwkv_w_up, rwkv_a0, rwkv_a_up, rwkv_g_up,
              rwkv_k_k, rwkv_k_a, rwkv_r_k, rwkv_gn_w, rwkv_gn_b,
              od_w_in, od_w_out, gdn_conv_w, gdn_a_log, gdn_dt_bias, gdn_norm_w):
    P = dict(norm_w=norm_w, final_norm_w=final_norm_w, ffn_gate=ffn_gate, ffn_up=ffn_up, ffn_down=ffn_down,
             ev_w_in=ev_w_in, ev_w_out=ev_w_out, rwkv_mu=rwkv_mu, rwkv_w0=rwkv_w0, rwkv_w_up=rwkv_w_up,
             rwkv_a0=rwkv_a0, rwkv_a_up=rwkv_a_up, rwkv_g_up=rwkv_g_up, rwkv_k_k=rwkv_k_k, rwkv_k_a=rwkv_k_a,
             rwkv_r_k=rwkv_r_k, rwkv_gn_w=rwkv_gn_w, rwkv_gn_b=rwkv_gn_b, od_w_in=od_w_in, od_w_out=od_w_out,
             gdn_conv_w=gdn_conv_w, gdn_a_log=gdn_a_log, gdn_dt_bias=gdn_dt_bias, gdn_norm_w=gdn_norm_w)
    C = dict(state_rwkv=state_rwkv, state_rwkv_shift=state_rwkv_shift, cache_moba_k=cache_moba_k,
             cache_moba_v=cache_moba_v, cache_dsa_k=cache_dsa_k, cache_dsa_v=cache_dsa_v,
             cache_dsa_kidx=cache_dsa_kidx, state_gdn=state_gdn, state_gdn_conv=state_gdn_conv)
    pos_p = jnp.arange(x_prompt.shape[1], dtype=jnp.int32)
    pos_s = PAST_LEN + jnp.arange(x_sample.shape[1], dtype=jnp.int32)
    y_prompt, ev_p, od_p = _trunk(x_prompt, pos_p, P, None, None)
    y_sample, ev_s, od_s = _trunk(x_sample, pos_s, P, C, page_table)
    rwkv_state_p, rwkv_shift_p, moba_k_p, moba_v_p = ev_p
    rwkv_state_s, rwkv_shift_s, moba_k_s, moba_v_s = ev_s
    dsa_k_p, dsa_v_p, dsa_kidx_p, gdn_state_p, gdn_conv_p = od_p
    dsa_k_s, dsa_v_s, dsa_kidx_s, gdn_state_s, gdn_conv_s = od_s
    return (y_prompt, y_sample,
            rwkv_state_p, rwkv_state_s, rwkv_shift_p, rwkv_shift_s,
            moba_k_p, moba_k_s, moba_v_p, moba_v_s,
            dsa_k_p, dsa_k_s, dsa_v_p, dsa_v_s, dsa_kidx_p, dsa_kidx_s,
            gdn_state_p, gdn_state_s, gdn_conv_p, gdn_conv_s)
```

```python
import functools
import math

import jax
import jax.numpy as jnp
from jax import lax
from jax.experimental import pallas as pl
from jax.experimental.pallas import tpu as pltpu

D_MODEL = 1024
DEPTH = 2
PAST_LEN = 2048
PAGE_SIZE = 128
HEAD_DIM = 64
GROUP_WIDTH = 512
N_HEADS = 8
D_FF = 2816
ROPE_THETA = 10000.0
NORM_EPS = 1e-6
ATTN_SCALE = HEAD_DIM ** -0.5
NEG = -1e30
Q_BLOCK = 128

RWKV_LORA_W = 64
RWKV_LORA_A = 64
RWKV_LORA_G = 128
RWKV_GN_EPS = 64e-5
RWKV_SIZES = (GROUP_WIDTH, RWKV_LORA_W, GROUP_WIDTH, GROUP_WIDTH, RWKV_LORA_A, RWKV_LORA_G)
RWKV_COLS = sum(RWKV_SIZES)
MOBA_BLOCK = 256
MOBA_TOPK = 3
EVEN_COLS = RWKV_COLS + 3 * GROUP_WIDTH
DSA_TOPK = 256
IDX_HEADS = 4
IDX_DIM = 64
IDX_W_SCALE = (IDX_HEADS * IDX_DIM) ** -0.5
DSA_SIZES = (GROUP_WIDTH, GROUP_WIDTH, GROUP_WIDTH, IDX_HEADS * IDX_DIM, IDX_DIM, IDX_HEADS)
DSA_COLS = sum(DSA_SIZES)
CONV_W = 4
GDN_CHUNK = 64
GDN_CONV_CH = 3 * GROUP_WIDTH
GDN_SIZES = (GDN_CONV_CH, GROUP_WIDTH, N_HEADS, N_HEADS)
ODD_COLS = DSA_COLS + sum(GDN_SIZES)

LANE = 128
VMEM_LIMIT = 56 * 1024 * 1024

BF16 = jnp.bfloat16
F32 = jnp.float32


def _params(*sem):
    return pltpu.CompilerParams(dimension_semantics=sem, vmem_limit_bytes=VMEM_LIMIT)


def _rms(x, g):
    return x * lax.rsqrt(jnp.mean(x * x, axis=-1, keepdims=True) + NORM_EPS) * g


def _ffn_kernel(x_ref, g_ref, wg_ref, wu_ref, wd_ref, o_ref, h_ref, acc_ref):
    f = pl.program_id(1)

    @pl.when(f == 0)
    def _():
        h_ref[...] = _rms(x_ref[...], g_ref[...]).astype(BF16)

    h = h_ref[...]
    a = jnp.dot(h, wg_ref[...], preferred_element_type=F32)
    b = jnp.dot(h, wu_ref[...], preferred_element_type=F32)
    act = (a * jax.nn.sigmoid(a) * b).astype(BF16)
    part = jnp.dot(act, wd_ref[...], preferred_element_type=F32)

    @pl.when(f == 0)
    def _():
        acc_ref[...] = part

    @pl.when(f > 0)
    def _():
        acc_ref[...] += part

    @pl.when(f == pl.num_programs(1) - 1)
    def _():
        o_ref[...] = x_ref[...] + 0.5 * acc_ref[...]


def _ffn(x, g, wg, wu, wd, tm, tf=256):
    m = x.shape[0]
    return pl.pallas_call(
        _ffn_kernel,
        out_shape=jax.ShapeDtypeStruct((m, D_MODEL), F32),
        grid=(m // tm, D_FF // tf),
        in_specs=[
            pl.BlockSpec((tm, D_MODEL), lambda i, f: (i, 0)),
            pl.BlockSpec((1, D_MODEL), lambda i, f: (0, 0)),
            pl.BlockSpec((D_MODEL, tf), lambda i, f: (0, f)),
            pl.BlockSpec((D_MODEL, tf), lambda i, f: (0, f)),
            pl.BlockSpec((tf, D_MODEL), lambda i, f: (f, 0)),
        ],
        out_specs=pl.BlockSpec((tm, D_MODEL), lambda i, f: (i, 0)),
        scratch_shapes=[pltpu.VMEM((tm, D_MODEL), BF16), pltpu.VMEM((tm, D_MODEL), F32)],
        compiler_params=_params("parallel", "arbitrary"),
        name="ffn",
    )(x, g.reshape(1, D_MODEL), wg, wu, wd)


def _proj_kernel(x_ref, g_ref, w_ref, o_ref, h_ref):
    @pl.when(pl.program_id(1) == 0)
    def _():
        h_ref[...] = _rms(x_ref[...], g_ref[...]).astype(BF16)

    o_ref[...] = jnp.dot(h_ref[...], w_ref[...], preferred_element_type=F32)


def _proj(x, g, w, tm, tn):
    m, n = x.shape[0], w.shape[1]
    return pl.pallas_call(
        _proj_kernel,
        out_shape=jax.ShapeDtypeStruct((m, n), F32),
        grid=(m // tm, n // tn),
        in_specs=[
            pl.BlockSpec((tm, D_MODEL), lambda i, j: (i, 0)),
            pl.BlockSpec((1, D_MODEL), lambda i, j: (0, 0)),
            pl.BlockSpec((D_MODEL, tn), lambda i, j: (0, j)),
        ],
        out_specs=pl.BlockSpec((tm, tn), lambda i, j: (i, j)),
        scratch_shapes=[pltpu.VMEM((tm, D_MODEL), BF16)],
        compiler_params=_params("parallel", "arbitrary"),
        name="in_proj",
    )(x, g.reshape(1, D_MODEL), w)


def _outproj_kernel(ya_ref, yb_ref, wa_ref, wb_ref, x_ref, o_ref):
    o_ref[...] = (x_ref[...]
                  + jnp.dot(ya_ref[...].astype(BF16), wa_ref[...], preferred_element_type=F32)
                  + jnp.dot(yb_ref[...].astype(BF16), wb_ref[...], preferred_element_type=F32))


def _outproj(ya, yb, w, x, tm):
    m = x.shape[0]
    return pl.pallas_call(
        _outproj_kernel,
        out_shape=jax.ShapeDtypeStruct((m, D_MODEL), F32),
        grid=(m // tm,),
        in_specs=[
            pl.BlockSpec((tm, GROUP_WIDTH), lambda i: (i, 0)),
            pl.BlockSpec((tm, GROUP_WIDTH), lambda i: (i, 0)),
            pl.BlockSpec((GROUP_WIDTH, D_MODEL), lambda i: (0, 0)),
            pl.BlockSpec((GROUP_WIDTH, D_MODEL), lambda i: (1, 0)),
            pl.BlockSpec((tm, D_MODEL), lambda i: (i, 0)),
        ],
        out_specs=pl.BlockSpec((tm, D_MODEL), lambda i: (i, 0)),
        compiler_params=_params("parallel"),
        name="out_proj",
    )(ya, yb, w, w, x)


def _norm_kernel(x_ref, g_ref, o_ref):
    o_ref[...] = _rms(x_ref[...], g_ref[...])


def _final_norm(x, g, tm):
    m = x.shape[0]
    return pl.pallas_call(
        _norm_kernel,
        out_shape=jax.ShapeDtypeStruct((m, D_MODEL), F32),
        grid=(m // tm,),
        in_specs=[pl.BlockSpec((tm, D_MODEL), lambda i: (i, 0)),
                  pl.BlockSpec((1, D_MODEL), lambda i: (0, 0))],
        out_specs=pl.BlockSpec((tm, D_MODEL), lambda i: (i, 0)),
        compiler_params=_params("parallel"),
        name="final_norm",
    )(x, g.reshape(1, D_MODEL))


def _split(z, sizes):
    out, o = [], 0
    for s in sizes:
        out.append(z[..., o:o + s])
        o += s
    return out


def _rope(x, pos):
    d = x.shape[-1]
    inv = ROPE_THETA ** (-jnp.arange(0, d, 2, dtype=jnp.float32) / d)
    ang = pos.astype(jnp.float32)[:, None] * inv[None, :]
    cos, sin = jnp.cos(ang)[:, None, :], jnp.sin(ang)[:, None, :]
    xf = x.astype(jnp.float32)
    x1, x2 = xf[..., : d // 2], xf[..., d // 2:]
    return jnp.concatenate([x1 * cos - x2 * sin, x2 * cos + x1 * sin], axis=-1).astype(x.dtype)


def _l2n(x):
    return x * lax.rsqrt(jnp.sum(x * x, axis=-1, keepdims=True) + 1e-6)


def _rwkv7_mix(z, z_prev, s0, i, P):
    f32 = jnp.float32
    N, T, _ = z.shape
    zs = jnp.concatenate([z_prev[:, None, :].astype(z.dtype), z[:, :-1]], axis=1)
    zm = z + (zs - z) * P['rwkv_mu'][i]
    r, wl, k, v, al, gl = _split(zm, RWKV_SIZES)
    w = -jax.nn.softplus(-(P['rwkv_w0'][i] + jnp.tanh(wl) @ P['rwkv_w_up'][i])) - 0.5
    decay = jnp.exp(-jnp.exp(w.astype(f32)))
    a = jax.nn.sigmoid(P['rwkv_a0'][i] + al @ P['rwkv_a_up'][i])
    g = jax.nn.sigmoid(gl) @ P['rwkv_g_up'][i]
    hd = lambda t: t.astype(f32).reshape(N, T, N_HEADS, HEAD_DIM)
    r, k, v, a, decay = hd(r), hd(k), hd(v), hd(a), hd(decay)
    kk = k * P['rwkv_k_k'][i].astype(f32).reshape(N_HEADS, HEAD_DIM)
    kk = kk / jnp.maximum(jnp.sqrt(jnp.sum(kk * kk, axis=-1, keepdims=True)), 1e-12)
    k = k * (1.0 + (a - 1.0) * P['rwkv_k_a'][i].astype(f32).reshape(N_HEADS, HEAD_DIM))

    def step(S, inp):
        r_t, w_t, k_t, v_t, kk_t, a_t = inp
        s_kk = jnp.einsum('nhij,nhj->nhi', S, -kk_t)
        S = (S * w_t[:, :, None, :] + s_kk[..., None] * (kk_t * a_t)[:, :, None, :]
             + v_t[..., None] * k_t[:, :, None, :])
        return S, jnp.einsum('nhij,nhj->nhi', S, r_t)

    tm = lambda t: jnp.swapaxes(t, 0, 1)
    S, y = lax.scan(step, s0.astype(f32), (tm(r), tm(decay), tm(k), tm(v), tm(kk), tm(a)))
    y = tm(y)
    mu = jnp.mean(y, axis=-1, keepdims=True)
    var = jnp.mean(jnp.square(y - mu), axis=-1, keepdims=True)
    y = ((y - mu) * lax.rsqrt(var + RWKV_GN_EPS)).reshape(N, T, GROUP_WIDTH)
    y = y * P['rwkv_gn_w'][i].astype(f32) + P['rwkv_gn_b'][i].astype(f32)
    bonus = jnp.sum(r * k * P['rwkv_r_k'][i].astype(f32), axis=-1, keepdims=True) * v
    y = (y + bonus.reshape(N, T, GROUP_WIDTH)) * g.astype(f32)
    return y.astype(z.dtype), S.astype(s0.dtype), z[:, -1]


def _moba_blocks(k, v):
    N, L, H, D = k.shape
    nb = -(-L // MOBA_BLOCK)
    pad = nb * MOBA_BLOCK - L

    def blocks(t):
        t = jnp.pad(t, ((0, 0), (0, pad), (0, 0), (0, 0)))
        return t.reshape(N, nb, MOBA_BLOCK, H, D).transpose(0, 3, 1, 2, 4)

    kbh, vbh = blocks(k), blocks(v)
    return kbh, vbh, jnp.mean(kbh.astype(jnp.float32), axis=3)


def _moba_attend(q, kbh, vbh, kmean, q_pos):
    f32 = jnp.float32
    N, Q, H, D = q.shape
    nb = kbh.shape[2]
    qf = q.astype(f32)
    own = q_pos // MOBA_BLOCK
    gate = jnp.einsum('nqhd,nhbd->nhqb', qf, kmean)
    gate = jnp.where(jnp.arange(nb)[None, :] < own[:, None], gate, NEG)
    kt = min(MOBA_TOPK, nb)
    _, top = lax.top_k(gate, kt)
    sel_ok = jnp.arange(kt)[None, :] < jnp.minimum(own, kt)[:, None]
    blk = jnp.concatenate([top, jnp.broadcast_to(own[None, None, :, None], (N, H, Q, 1))], axis=-1)
    n_i = jnp.arange(N)[:, None, None, None]
    h_i = jnp.arange(H)[None, :, None, None]
    kg = kbh[n_i, h_i, blk].astype(f32)
    vg = vbh[n_i, h_i, blk].astype(f32)
    own_ok = (own[:, None] * MOBA_BLOCK + jnp.arange(MOBA_BLOCK)[None, :]) <= q_pos[:, None]
    ok = jnp.concatenate([jnp.broadcast_to(sel_ok[:, :, None], (Q, kt, MOBA_BLOCK)), own_ok[:, None, :]], axis=1)
    lg = jnp.einsum('nqhd,nhqsld->nhqsl', qf, kg) * ATTN_SCALE
    lg = jnp.where(ok[None, None], lg, -jnp.inf)
    p = jax.nn.softmax(lg.reshape(N, H, Q, -1), axis=-1).reshape(lg.shape)
    return jnp.einsum('nhqsl,nhqsld->nqhd', p, vg).astype(q.dtype)


def _moba_prompt(q, k, v):
    B, T = q.shape[:2]
    kbh, vbh, kmean = _moba_blocks(k, v)
    nq = T // Q_BLOCK
    qc = jnp.swapaxes(q.reshape((B, nq, Q_BLOCK) + q.shape[2:]), 0, 1)
    pos = jnp.arange(T, dtype=jnp.int32).reshape(nq, Q_BLOCK)
    out = lax.map(lambda a: _moba_attend(a[0], kbh, vbh, kmean, a[1]), (qc, pos))
    return jnp.swapaxes(out, 0, 1).reshape(q.shape)


def _moba_sample(q, k, v, ck, cv, page_table):
    N, Q = q.shape[:2]
    past = lambda c: c[page_table].reshape((N, PAST_LEN) + c.shape[2:])
    k_all = jnp.concatenate([past(ck).astype(k.dtype), k], axis=1)
    v_all = jnp.concatenate([past(cv).astype(v.dtype), v], axis=1)
    kbh, vbh, kmean = _moba_blocks(k_all, v_all)
    return _moba_attend(q, kbh, vbh, kmean, PAST_LEN + jnp.arange(Q, dtype=jnp.int32))


def _index_scores(qi, wi, ki):
    s = jax.nn.relu(jnp.einsum('nqhd,nld->nqhl', qi.astype(jnp.float32), ki.astype(jnp.float32)))
    return jnp.einsum('nqh,nqhl->nql', wi.astype(jnp.float32), s)


def _dsa_select(scores, q_pos, ksel):
    L = scores.shape[-1]
    adm = jnp.arange(L)[None, :] <= q_pos[:, None]
    _, idx = lax.top_k(jnp.where(adm[None], scores, NEG), ksel)
    return idx, idx <= q_pos[None, :, None]


def _sparse_attend(q, kg, vg, ok):
    f32 = jnp.float32
    lg = jnp.einsum('nqhd,nqkhd->nhqk', q.astype(f32), kg.astype(f32)) * ATTN_SCALE
    p = jax.nn.softmax(jnp.where(ok[:, None], lg, -jnp.inf), axis=-1)
    return jnp.einsum('nhqk,nqkhd->nqhd', p, vg.astype(f32)).astype(q.dtype)


def _dsa_prompt(q, k, v, qi, ki, wi):
    B, T = q.shape[:2]
    ksel = min(DSA_TOPK, T // 4)
    nq = T // Q_BLOCK
    blk = lambda t: jnp.swapaxes(t.reshape((B, nq, Q_BLOCK) + t.shape[2:]), 0, 1)
    pos = jnp.arange(T, dtype=jnp.int32).reshape(nq, Q_BLOCK)
    b_i = jnp.arange(B)[:, None, None]

    def one(args):
        qc, qic, wic, pc = args
        idx, ok = _dsa_select(_index_scores(qic, wic, ki), pc, ksel)
        return _sparse_attend(qc, k[b_i, idx], v[b_i, idx], ok)

    out = lax.map(one, (blk(q), blk(qi), blk(wi), pos))
    return jnp.swapaxes(out, 0, 1).reshape(q.shape)


def _dsa_sample(q, k, v, qi, ki, wi, ck, cv, cki, page_table):
    N, Q = q.shape[:2]
    ki_all = jnp.concatenate([cki[page_table].reshape(N, PAST_LEN, IDX_DIM).astype(ki.dtype), ki], axis=1)
    ksel = min(DSA_TOPK, (PAST_LEN + Q) // 4)
    pos = PAST_LEN + jnp.arange(Q, dtype=jnp.int32)
    idx, ok = _dsa_select(_index_scores(qi, wi, ki_all), pos, ksel)
    n_i = jnp.arange(N)[:, None, None]
    in_past = (idx < PAST_LEN)[..., None, None]
    pidx = jnp.minimum(idx, PAST_LEN - 1)
    phys = page_table[n_i, pidx // PAGE_SIZE]
    off = pidx % PAGE_SIZE
    nidx = jnp.clip(idx - PAST_LEN, 0, Q - 1)
    kg = jnp.where(in_past, ck[phys, off].astype(k.dtype), k[n_i, nidx])
    vg = jnp.where(in_past, cv[phys, off].astype(v.dtype), v[n_i, nidx])
    return _sparse_attend(q, kg, vg, ok)


def _chunk_gated_delta(q, k, v, g, beta, s0):
    N, T, H, D = q.shape
    C = min(GDN_CHUNK, T)
    nc = -(-T // C)
    pad = nc * C - T

    def prep(t):
        t = jnp.pad(t, ((0, 0), (0, pad)) + ((0, 0),) * (t.ndim - 2))
        return jnp.moveaxis(t.reshape((N, nc, C) + t.shape[2:]), 3, 1)

    q, k, v, g, beta = prep(q), prep(k), prep(v), prep(g), prep(beta)
    G = jnp.cumsum(g, axis=-1)
    tri = jnp.tril(jnp.ones((C, C), bool))
    strict = jnp.tril(jnp.ones((C, C), bool), -1)
    decay = jnp.exp(jnp.where(tri, G[..., :, None] - G[..., None, :], -jnp.inf))
    kb = k * beta[..., None]
    A = jnp.where(strict, jnp.einsum('nhcid,nhcjd->nhcij', kb, k) * decay, 0.0)
    eye = jnp.broadcast_to(jnp.eye(C, dtype=A.dtype), A.shape)
    Tm = lax.linalg.triangular_solve(eye + A, eye, left_side=True, lower=True, unit_diagonal=True)
    eG = jnp.exp(G)[..., None]
    U = Tm @ (v * beta[..., None])
    W = Tm @ (kb * eG)
    attn = jnp.einsum('nhcid,nhcjd->nhcij', q, k) * decay
    qg = q * eG
    kd = k * jnp.exp(G[..., -1:] - G)[..., None]
    gl = jnp.exp(G[..., -1])

    def step(S, xs):
        U_c, W_c, a_c, qg_c, kd_c, gl_c = xs
        vn = U_c - W_c @ S
        o = qg_c @ S + a_c @ vn
        S = S * gl_c[..., None, None] + jnp.swapaxes(kd_c, -1, -2) @ vn
        return S, o

    cm = lambda t: jnp.moveaxis(t, 2, 0)
    S, o = lax.scan(step, s0, (cm(U), cm(W), cm(attn), cm(qg), cm(kd), cm(gl)))
    o = jnp.moveaxis(jnp.moveaxis(o, 0, 2), 1, 3).reshape(N, nc * C, H, D)[:, :T]
    return o, S


def _gdn_mix(z, buf, s0, i, P):
    f32 = jnp.float32
    N, T, _ = z.shape
    qkv, zg, b, a = _split(z, GDN_SIZES)
    ext = jnp.concatenate([buf.astype(qkv.dtype), qkv], axis=1)
    conv = lax.conv_general_dilated(ext, P['gdn_conv_w'][i][:, None, :].astype(ext.dtype), (1,), 'VALID',
                                    dimension_numbers=('NWC', 'WIO', 'NWC'), feature_group_count=GDN_CONV_CH)
    conv = jax.nn.silu(conv.astype(f32))
    hd = lambda t: t.reshape(N, T, N_HEADS, HEAD_DIM)
    q, k, v = (hd(t) for t in _split(conv, (GROUP_WIDTH,) * 3))
    q = _l2n(q) * HEAD_DIM ** -0.5
    k = _l2n(k)
    beta = jax.nn.sigmoid(b.astype(f32))
    gdec = -jnp.exp(P['gdn_a_log'][i].astype(f32)) * jax.nn.softplus(a.astype(f32) + P['gdn_dt_bias'][i].astype(f32))
    o, S = _chunk_gated_delta(q, k, v, gdec, beta, s0.astype(f32))
    o = o * lax.rsqrt(jnp.mean(o * o, axis=-1, keepdims=True) + NORM_EPS) * P['gdn_norm_w'][i].astype(f32)
    o = o * jax.nn.silu(hd(zg).astype(f32))
    return o.reshape(N, T, GROUP_WIDTH).astype(z.dtype), S.astype(s0.dtype), ext[:, -(CONV_W - 1):]


def _even_mixer(z, pos, i, P, shift0, s0, moba_fn):
    N, T, _ = z.shape
    y_a, s_new, shift_new = _rwkv7_mix(z[..., :RWKV_COLS], shift0, s0, i, P)
    q, k, v = (t.reshape(N, T, N_HEADS, HEAD_DIM) for t in _split(z[..., RWKV_COLS:], (GROUP_WIDTH,) * 3))
    q, k = _rope(q, pos), _rope(k, pos)
    y_b = moba_fn(q, k, v).reshape(N, T, GROUP_WIDTH)
    return y_a, y_b, (s_new, shift_new, k, v)


def _odd_mixer(z, pos, i, P, conv0, s0, dsa_fn):
    N, T, _ = z.shape
    q, k, v, qi, ki, wi = _split(z[..., :DSA_COLS], DSA_SIZES)
    hd = lambda t: t.reshape(N, T, N_HEADS, HEAD_DIM)
    q, k, v = _rope(hd(q), pos), _rope(hd(k), pos), hd(v)
    qi = _rope(qi.reshape(N, T, IDX_HEADS, IDX_DIM), pos)
    ki = _rope(ki[:, :, None, :], pos)[:, :, 0]
    y_c = dsa_fn(q, k, v, qi, ki, wi * IDX_W_SCALE).reshape(N, T, GROUP_WIDTH)
    y_d, s_new, conv_new = _gdn_mix(z[..., DSA_COLS:], conv0, s0, i, P)
    return y_c, y_d, (k, v, ki, s_new, conv_new)


def _pad_cols(w, mult):
    n = w.shape[1]
    pad = (-n) % mult
    return jnp.pad(w, ((0, 0), (0, pad))) if pad else w


def _trunk(x, pos, P, W, C, page_table, tm):
    N, T, _ = x.shape
    M = N * T
    x = x.reshape(M, D_MODEL)
    ev_states, od_states = [], []
    for l in range(DEPTH):
        i = l // 2
        x = _ffn(x, P['norm_w'][l, 0], W['ffn_gate'][l][0], W['ffn_up'][l][0], W['ffn_down'][l][0], tm)
        if l % 2 == 0:
            z = _proj(x, P['norm_w'][l, 1], W['ev_w_in'][i], tm, 256)[:, :EVEN_COLS].reshape(N, T, EVEN_COLS)
            if C is None:
                ya, yb, st = _even_mixer(z, pos, i, P, jnp.zeros((N, RWKV_COLS), F32),
                                         jnp.zeros((N, N_HEADS, HEAD_DIM, HEAD_DIM), F32), _moba_prompt)
            else:
                fn = functools.partial(_moba_sample, ck=C['cache_moba_k'][i], cv=C['cache_moba_v'][i], page_table=page_table)
                ya, yb, st = _even_mixer(z, pos, i, P, C['state_rwkv_shift'][i], C['state_rwkv'][i], fn)
            ev_states.append(st)
            w_out = W['ev_w_out'][i]
        else:
            z = _proj(x, P['norm_w'][l, 1], W['od_w_in'][i], tm, 256)[:, :ODD_COLS].reshape(N, T, ODD_COLS)
            if C is None:
                ya, yb, st = _odd_mixer(z, pos, i, P, jnp.zeros((N, CONV_W - 1, GDN_CONV_CH), F32),
                                        jnp.zeros((N, N_HEADS, HEAD_DIM, HEAD_DIM), F32), _dsa_prompt)
            else:
                fn = functools.partial(_dsa_sample, ck=C['cache_dsa_k'][i], cv=C['cache_dsa_v'][i],
                                       cki=C['cache_dsa_kidx'][i], page_table=page_table)
                ya, yb, st = _odd_mixer(z, pos, i, P, C['state_gdn_conv'][i], C['state_gdn'][i], fn)
            od_states.append(st)
            w_out = W['od_w_out'][i]
        x = _outproj(ya.reshape(M, GROUP_WIDTH), yb.reshape(M, GROUP_WIDTH), w_out, x, tm)
        x = _ffn(x, P['norm_w'][l, 2], W['ffn_gate'][l][1], W['ffn_up'][l][1], W['ffn_down'][l][1], tm)
    ev = [jnp.stack(c) for c in zip(*ev_states)]
    od = [jnp.stack(c) for c in zip(*od_states)]
    y = _final_norm(x, P['final_norm_w'], tm).reshape(N, T, D_MODEL)
    return y, ev, od


def kernel(x_prompt, x_sample, state_rwkv, state_rwkv_shift, cache_moba_k, cache_moba_v, cache_dsa_k, cache_dsa_v, cache_dsa_kidx, state_gdn, state_gdn_conv, page_table, norm_w, final_norm_w, ffn_gate, ffn_up, ffn_down, ev_w_in, ev_w_out, rwkv_mu, rwkv_w0, rwkv_w_up, rwkv_a0, rwkv_a_up, rwkv_g_up, rwkv_k_k, rwkv_k_a, rwkv_r_k, rwkv_gn_w, rwkv_gn_b, od_w_in, od_w_out, gdn_conv_w, gdn_a_log, gdn_dt_bias, gdn_norm_w):
    P = dict(norm_w=norm_w, final_norm_w=final_norm_w, rwkv_mu=rwkv_mu, rwkv_w0=rwkv_w0, rwkv_w_up=rwkv_w_up,
             rwkv_a0=rwkv_a0, rwkv_a_up=rwkv_a_up, rwkv_g_up=rwkv_g_up, rwkv_k_k=rwkv_k_k, rwkv_k_a=rwkv_k_a,
             rwkv_r_k=rwkv_r_k, rwkv_gn_w=rwkv_gn_w, rwkv_gn_b=rwkv_gn_b,
             gdn_conv_w=gdn_conv_w, gdn_a_log=gdn_a_log, gdn_dt_bias=gdn_dt_bias, gdn_norm_w=gdn_norm_w)
    W = dict(ffn_gate=ffn_gate.astype(BF16), ffn_up=ffn_up.astype(BF16), ffn_down=ffn_down.astype(BF16),
             ev_w_in=[_pad_cols(ev_w_in[i], 256).astype(BF16) for i in range(ev_w_in.shape[0])],
             od_w_in=[_pad_cols(od_w_in[i], 256).astype(BF16) for i in range(od_w_in.shape[0])],
             ev_w_out=ev_w_out.astype(BF16), od_w_out=od_w_out.astype(BF16))
    C = dict(state_rwkv=state_rwkv, state_rwkv_shift=state_rwkv_shift, cache_moba_k=cache_moba_k,
             cache_moba_v=cache_moba_v, cache_dsa_k=cache_dsa_k, cache_dsa_v=cache_dsa_v,
             cache_dsa_kidx=cache_dsa_kidx, state_gdn=state_gdn, state_gdn_conv=state_gdn_conv)
    pos_p = jnp.arange(x_prompt.shape[1], dtype=jnp.int32)
    pos_s = PAST_LEN + jnp.arange(x_sample.shape[1], dtype=jnp.int32)
    y_prompt, ev_p, od_p = _trunk(x_prompt, pos_p, P, W, None, None, 1024)
    y_sample, ev_s, od_s = _trunk(x_sample, pos_s, P, W, C, page_table, 128)
    rwkv_state_p, rwkv_shift_p, moba_k_p, moba_v_p = ev_p
    rwkv_state_s, rwkv_shift_s, moba_k_s, moba_v_s = ev_s
    dsa_k_p, dsa_v_p, dsa_kidx_p, gdn_state_p, gdn_conv_p = od_p
    dsa_k_s, dsa_v_s, dsa_kidx_s, gdn_state_s, gdn_conv_s = od_s
    return (y_prompt, y_sample,
            rwkv_state_p, rwkv_state_s, rwkv_shift_p, rwkv_shift_s,
            moba_k_p, moba_k_s, moba_v_p, moba_v_s,
            dsa_k_p, dsa_k_s, dsa_v_p, dsa_v_s, dsa_kidx_p, dsa_kidx_s,
            gdn_state_p, gdn_state_s, gdn_conv_p, gdn_conv_s)
```
